```python
import jax, jax.numpy as jnp
from jax import lax
import numpy as np

D_MODEL = 4096
BATCH = 1
SEQ = 8192
DEPTH = 1
DEC_BATCH = 128
DEC_SEQ = 4
PAST_LEN = 2048
PAGE_SIZE = 128

HEAD_DIM = 128
MIX_WIDTH = D_MODEL
N_HEADS_GLA = 4
DV_GLA = (MIX_WIDTH // 2) // N_HEADS_GLA
DK_GLA = DV_GLA // 2
GLA_LOWRANK = 16
GLA_TAU = 16.0
GLA_CHUNK = 64
N_HEADS_DSW = (MIX_WIDTH // 2) // HEAD_DIM
DILATION_GROUPS = ((128, 1), (512, 4), (2048, 16))
W_MAX = 2048
Q_BLOCK = 128
D_FF = ((8 * D_MODEL + 3 * 256 - 1) // (3 * 256)) * 256
ROPE_THETA = 10000.0
EPS = 1e-6

QK_GLA_COLS = N_HEADS_GLA * DK_GLA
V_GLA_COLS = N_HEADS_GLA * DV_GLA
DSW_COLS = N_HEADS_DSW * HEAD_DIM
IN_SPLITS = (
    QK_GLA_COLS,
    2 * QK_GLA_COLS,
    2 * QK_GLA_COLS + V_GLA_COLS,
    2 * QK_GLA_COLS + V_GLA_COLS + GLA_LOWRANK,
    2 * QK_GLA_COLS + 2 * V_GLA_COLS + GLA_LOWRANK,
    2 * QK_GLA_COLS + 2 * V_GLA_COLS + GLA_LOWRANK + DSW_COLS,
    2 * QK_GLA_COLS + 2 * V_GLA_COLS + GLA_LOWRANK + 2 * DSW_COLS,
)
IN_COLS = 2 * QK_GLA_COLS + 2 * V_GLA_COLS + GLA_LOWRANK + 3 * DSW_COLS
OUT_COLS = V_GLA_COLS + DSW_COLS

kernel_name = 'hymba_gla_dilated_swa_step'


def rmsnorm(x, g):
    xf = x.astype(jnp.float32)
    y = xf * lax.rsqrt(jnp.mean(xf * xf, axis=-1, keepdims=True) + EPS)
    return (y * g.astype(jnp.float32)).astype(x.dtype)


def rope(x, pos):
    half = HEAD_DIM // 2
    inv = ROPE_THETA ** (-jnp.arange(half, dtype=jnp.float32) / half)
    ang = pos.astype(jnp.float32)[:, None] * inv[None, :]
    cos = jnp.cos(ang)[:, None, :]
    sin = jnp.sin(ang)[:, None, :]
    xf = x.astype(jnp.float32)
    x1, x2 = xf[..., :half], xf[..., half:]
    return jnp.concatenate([x1 * cos - x2 * sin, x2 * cos + x1 * sin], axis=-1).astype(x.dtype)


def project(xn, pos, w_in, w_a2, b_a, qn_g, kn_g):
    b, t, _ = xn.shape
    z = xn @ w_in
    q_a, k_a, v_a, a_lr, r_a, q_d, k_d, v_d = jnp.split(z, IN_SPLITS, axis=-1)
    q_a = q_a.reshape(b, t, N_HEADS_GLA, DK_GLA) * (DK_GLA ** -0.5)
    k_a = k_a.reshape(b, t, N_HEADS_GLA, DK_GLA)
    v_a = v_a.reshape(b, t, N_HEADS_GLA, DV_GLA)
    log_a = (jax.nn.log_sigmoid((a_lr @ w_a2 + b_a).astype(jnp.float32)) / GLA_TAU).reshape(b, t, N_HEADS_GLA, DK_GLA)
    q_d = rope(rmsnorm(q_d.reshape(b, t, N_HEADS_DSW, HEAD_DIM), qn_g), pos)
    k_d = rope(rmsnorm(k_d.reshape(b, t, N_HEADS_DSW, HEAD_DIM), kn_g), pos)
    v_d = v_d.reshape(b, t, N_HEADS_DSW, HEAD_DIM)
    return q_a, k_a, v_a, log_a, r_a, q_d, k_d, v_d


def gla_chunked(q, k, v, log_a, s0):
    b, t, h, dk = q.shape
    dv = v.shape[-1]
    c = min(GLA_CHUNK, t)
    n = -(-t // c)
    pad = n * c - t

    def chunks(a):
        a = jnp.pad(a, ((0, 0), (0, pad), (0, 0), (0, 0)))
        return a.reshape(b, n, c, h, a.shape[-1]).transpose(1, 0, 3, 2, 4)

    causal = jnp.tril(jnp.ones((c, c), dtype=bool))

    def step(s, inp):
        qc, kc, vc, gc = inp
        qf = qc.astype(jnp.float32)
        kf = kc.astype(jnp.float32)
        vf = vc.astype(jnp.float32)
        bcum = jnp.cumsum(gc, axis=2)
        o_inter = jnp.einsum('bhtk,bhkv->bhtv', qf * jnp.exp(bcum), s)
        diff = bcum[:, :, :, None, :] - bcum[:, :, None, :, :]
        decay = jnp.exp(jnp.where(causal[:, :, None], diff, -jnp.inf))
        att = jnp.einsum('bhtk,bhsk,bhtsk->bhts', qf, kf, decay)
        o_intra = jnp.einsum('bhts,bhsv->bhtv', att, vf)
        b_last = bcum[:, :, -1:, :]
        s_new = jnp.exp(b_last[:, :, 0, :])[..., None] * s + jnp.einsum('bhsk,bhsv->bhkv', kf * jnp.exp(b_last - bcum), vf)
        return s_new, o_inter + o_intra

    s_fin, o = lax.scan(step, s0.astype(jnp.float32), (chunks(q), chunks(k), chunks(v), chunks(log_a)))
    o = o.transpose(1, 0, 3, 2, 4).reshape(b, n * c, h, dv)[:, :t]
    return o.astype(v.dtype), s_fin.astype(s0.dtype)


def combine_branches(outs, lses):
    wts = jax.nn.softmax(jnp.stack(lses, axis=0), axis=0)
    return jnp.sum(wts[..., None] * jnp.stack(outs, axis=0), axis=0)


def dswa_prompt(q, k, v):
    b, t, h, e = q.shape
    qb_len = min(Q_BLOCK, t)
    scale = HEAD_DIM ** -0.5
    kp = jnp.pad(k, ((0, 0), (W_MAX, 0), (0, 0), (0, 0)))
    vp = jnp.pad(v, ((0, 0), (W_MAX, 0), (0, 0), (0, 0)))

    def block(t0):
        qblk = lax.dynamic_slice_in_dim(q, t0, qb_len, axis=1)
        outs, lses = [], []
        for win, dil in DILATION_GROUPS:
            nq = qb_len // dil
            nk = (win + qb_len) // dil
            kblk = lax.dynamic_slice_in_dim(kp, t0 + W_MAX - win, win + qb_len, axis=1).reshape(b, nk, dil, h, e)
            vblk = lax.dynamic_slice_in_dim(vp, t0 + W_MAX - win, win + qb_len, axis=1).reshape(b, nk, dil, h, e)
            qr = qblk.reshape(b, nq, dil, h, e)
            s = jnp.einsum('birhe,bmrhe->bhrim', qr, kblk).astype(jnp.float32) * scale
            qi = jnp.arange(nq)[None, :, None]
            km = jnp.arange(nk)[None, None, :]
            rr = jnp.arange(dil)[:, None, None]
            ktime = t0 - win + km * dil + rr
            valid = (km >= qi) & (km <= qi + win // dil) & (ktime >= 0)
            s = jnp.where(valid, s, -jnp.inf)
            lse = jax.nn.logsumexp(s, axis=-1)
            p = jnp.exp(s - lse[..., None])
            o = jnp.einsum('bhrim,bmrhe->birhe', p, vblk.astype(jnp.float32)).reshape(b, qb_len, h, e)
            outs.append(o)
            lses.append(jnp.transpose(lse, (0, 3, 2, 1)).reshape(b, qb_len, h))
        return combine_branches(outs, lses).astype(q.dtype)

    starts = jnp.arange(t // qb_len) * qb_len
    out = lax.map(block, starts)
    return jnp.transpose(out, (1, 0, 2, 3, 4)).reshape(b, t, h, e)


def dswa_sample(q, k_new, v_new, cache_k, cache_v):
    b, l, h, e = q.shape
    lw = cache_k.shape[1]
    scale = HEAD_DIM ** -0.5
    j = jnp.arange(l)
    s_new_all = jnp.einsum('bjhe,bnhe->bhjn', q, k_new).astype(jnp.float32) * scale
    delta = j[:, None] - j[None, :]
    outs, lses = [], []
    for win, dil in DILATION_GROUPS:
        m_cnt = win // dil + 1
        idx = lw + j[:, None] - dil * jnp.arange(m_cnt)[None, :]
        valid_c = (idx >= 0) & (idx < lw)
        idxc = jnp.clip(idx, 0, lw - 1)
        kg = cache_k[:, idxc]
        vg = cache_v[:, idxc]
        s_c = jnp.einsum('bjhe,bjmhe->bhjm', q, kg).astype(jnp.float32) * scale
        s_c = jnp.where(valid_c[None, None], s_c, -jnp.inf)
        valid_n = (delta >= 0) & (delta % dil == 0) & (delta <= win)
        s_n = jnp.where(valid_n[None, None], s_new_all, -jnp.inf)
        s = jnp.concatenate([s_c, s_n], axis=-1)
        lse = jax.nn.logsumexp(s, axis=-1)
        p = jnp.exp(s - lse[..., None])
        o = (jnp.einsum('bhjm,bjmhe->bjhe', p[..., :m_cnt], vg.astype(jnp.float32))
             + jnp.einsum('bhjn,bnhe->bjhe', p[..., m_cnt:], v_new.astype(jnp.float32)))
        outs.append(o)
        lses.append(jnp.transpose(lse, (0, 2, 1)))
    return combine_branches(outs, lses).astype(q.dtype)


def finish(x, o_a, r_a, o_d, gla_norm_g, w_out, norm2_g, w_gate, w_up, w_down):
    b, t, _ = x.shape
    o_a = rmsnorm(o_a, gla_norm_g).reshape(b, t, V_GLA_COLS) * jax.nn.silu(r_a)
    mix = jnp.concatenate([o_a, o_d.reshape(b, t, DSW_COLS)], axis=-1)
    h = x + mix @ w_out
    hn = rmsnorm(h, norm2_g)
    return h + (jax.nn.silu(hn @ w_gate) * (hn @ w_up)) @ w_down


def setup_inputs(seed: int = 0) -> dict:
    key = jax.random.key(seed)
    ks = jax.random.split(key, 20)
    nrm = jax.random.normal
    f32 = jnp.float32
    lw = min(W_MAX, PAST_LEN)
    return {
        'x_prompt': nrm(ks[0], (BATCH, SEQ, D_MODEL), f32),
        'x_sample': nrm(ks[1], (DEC_BATCH, DEC_SEQ, D_MODEL), f32),
        'cache_k': nrm(ks[2], (DEPTH, DEC_BATCH, lw, N_HEADS_DSW, HEAD_DIM), f32),
        'cache_v': nrm(ks[3], (DEPTH, DEC_BATCH, lw, N_HEADS_DSW, HEAD_DIM), f32),
        'state_gla': nrm(ks[4], (DEPTH, DEC_BATCH, N_HEADS_GLA, DK_GLA, DV_GLA), f32),
        'norm1_g': 1.0 + 0.02 * nrm(ks[5], (DEPTH, D_MODEL), f32),
        'w_in': nrm(ks[6], (DEPTH, D_MODEL, IN_COLS), f32) * D_MODEL ** -0.5,
        'w_a2': nrm(ks[7], (DEPTH, GLA_LOWRANK, QK_GLA_COLS), f32) * GLA_LOWRANK ** -0.5,
        'b_a': 0.1 * nrm(ks[8], (DEPTH, QK_GLA_COLS), f32),
        'qn_g': 1.0 + 0.02 * nrm(ks[9], (DEPTH, HEAD_DIM), f32),
        'kn_g': 1.0 + 0.02 * nrm(ks[10], (DEPTH, HEAD_DIM), f32),
        'gla_norm_g': 1.0 + 0.02 * nrm(ks[11], (DEPTH, DV_GLA), f32),
        'w_out': nrm(ks[12], (DEPTH, OUT_COLS, D_MODEL), f32) * OUT_COLS ** -0.5,
        'norm2_g': 1.0 + 0.02 * nrm(ks[13], (DEPTH, D_MODEL), f32),
        'w_gate': nrm(ks[14], (DEPTH, D_MODEL, D_FF), f32) * D_MODEL ** -0.5,
        'w_up': nrm(ks[15], (DEPTH, D_MODEL, D_FF), f32) * D_MODEL ** -0.5,
        'w_down': nrm(ks[16], (DEPTH, D_FF, D_MODEL), f32) * D_FF ** -0.5,
    }


def reference(x_prompt, x_sample, cache_k, cache_v, state_gla, norm1_g, w_in, w_a2, b_a, qn_g, kn_g,
              gla_norm_g, w_out, norm2_g, w_gate, w_up, w_down):
    bp, tp, _ = x_prompt.shape
    ls = x_sample.shape[1]
    pos_p = jnp.arange(tp)
    pos_s = PAST_LEN + jnp.arange(ls)
    lwp = min(W_MAX, tp)
    yp, ys = x_prompt, x_sample
    kp_l, vp_l, sp_l, ks_l, vs_l, ss_l = [], [], [], [], [], []
    for li in range(DEPTH):
        xn = rmsnorm(yp, norm1_g[li])
        qa, ka, va, la, ra, qd, kd, vd = project(xn, pos_p, w_in[li], w_a2[li], b_a[li], qn_g[li], kn_g[li])
        s0 = jnp.zeros((bp, N_HEADS_GLA, DK_GLA, DV_GLA), jnp.float32)
        oa, s_p = gla_chunked(qa, ka, va, la, s0)
        od = dswa_prompt(qd, kd, vd)
        yp = finish(yp, oa, ra, od, gla_norm_g[li], w_out[li], norm2_g[li], w_gate[li], w_up[li], w_down[li])
        kp_l.append(kd[:, tp - lwp:])
        vp_l.append(vd[:, tp - lwp:])
        sp_l.append(s_p)
        xn = rmsnorm(ys, norm1_g[li])
        qa, ka, va, la, ra, qd, kd, vd = project(xn, pos_s, w_in[li], w_a2[li], b_a[li], qn_g[li], kn_g[li])
        oa, s_s = gla_chunked(qa, ka, va, la, state_gla[li])
        od = dswa_sample(qd, kd, vd, cache_k[li], cache_v[li])
        ys = finish(ys, oa, ra, od, gla_norm_g[li], w_out[li], norm2_g[li], w_gate[li], w_up[li], w_down[li])
        ks_l.append(kd)
        vs_l.append(vd)
        ss_l.append(s_s)
    new_cache_k_prompt = jnp.stack(kp_l, axis=0)
    new_cache_v_prompt = jnp.stack(vp_l, axis=0)
    new_state_gla_prompt = jnp.stack(sp_l, axis=0)
    new_cache_k_sample = jnp.stack(ks_l, axis=0)
    new_cache_v_sample = jnp.stack(vs_l, axis=0)
    new_state_gla_sample = jnp.stack(ss_l, axis=0)
    return (yp, ys, new_cache_k_prompt, new_cache_v_prompt, new_state_gla_prompt,
            new_cache_k_sample, new_cache_v_sample, new_state_gla_sample)
```

```python
import functools
import math

import numpy as np
import jax
import jax.numpy as jnp
from jax import lax
from jax.experimental import pallas as pl
from jax.experimental.pallas import tpu as pltpu

F32 = jnp.float32
BF16 = jnp.bfloat16

D_MODEL = 4096
HEAD_DIM = 128
N_HEADS_GLA = 4
DV_GLA = 512
DK_GLA = 256
GLA_LOWRANK = 16
GLA_TAU = 16.0
N_HEADS_DSW = 16
W_MAX = 2048
PAST_LEN = 2048
D_FF = 11008
ROPE_THETA = 10000.0
EPS = 1e-6
QK_GLA_COLS = N_HEADS_GLA * DK_GLA
V_GLA_COLS = N_HEADS_GLA * DV_GLA
DSW_COLS = N_HEADS_DSW * HEAD_DIM
ALR_OFF = 2 * QK_GLA_COLS + V_GLA_COLS
DSW_SCALE = HEAD_DIM ** -0.5

LANES = 128
VMEM_LIMIT = 56 * 1024 * 1024

SEC_QKV_GLA = (0, 4)
SEC_R_GLA = (4, 2)
SEC_QK_DSW = (6, 4)
SEC_V_DSW = (10, 2)
PROJ_TN = 1024


def _dot(a, b):
    return jnp.dot(a, b, preferred_element_type=F32)


def _dot_nt(a, b):
    return lax.dot_general(a, b, (((1,), (1,)), ((), ())), preferred_element_type=F32)


def _dot_tn(a, b):
    return lax.dot_general(a, b, (((0,), (0,)), ((), ())), preferred_element_type=F32)


def _params(*sem):
    return pltpu.CompilerParams(dimension_semantics=sem, vmem_limit_bytes=VMEM_LIMIT)


def _prenorm_kernel(x_ref, g_ref, *rest, with_gate):
    x = x_ref[...]
    ms = jnp.mean(x * x, axis=-1, keepdims=True)
    xn = (x * lax.rsqrt(ms + EPS) * g_ref[...]).astype(BF16)
    if not with_gate:
        (xn_ref,) = rest
        xn_ref[...] = xn
        return
    walr_ref, wa2_ref, ba_ref, xn_ref, la_ref = rest
    xn_ref[...] = xn
    a = _dot(xn, walr_ref[...])
    z = jnp.dot(a, wa2_ref[...], precision=lax.Precision.HIGHEST,
                preferred_element_type=F32) + ba_ref[...]
    la_ref[...] = (jnp.minimum(z, 0.0) - jnp.log1p(jnp.exp(-jnp.abs(z)))) * (1.0 / GLA_TAU)


def _prenorm(x, g, gate=None, tm=256):
    m, d = x.shape
    tm = min(tm, m)
    in_specs = [pl.BlockSpec((tm, d), lambda i: (i, 0)), pl.BlockSpec((1, d), lambda i: (0, 0))]
    out_shape = [jax.ShapeDtypeStruct((m, d), BF16)]
    out_specs = [pl.BlockSpec((tm, d), lambda i: (i, 0))]
    args = [x, g.reshape(1, d)]
    if gate is not None:
        walr, wa2, ba = gate
        in_specs += [pl.BlockSpec((d, LANES), lambda i: (0, 0)),
                     pl.BlockSpec((LANES, QK_GLA_COLS), lambda i: (0, 0)),
                     pl.BlockSpec((1, QK_GLA_COLS), lambda i: (0, 0))]
        out_shape.append(jax.ShapeDtypeStruct((m, QK_GLA_COLS), F32))
        out_specs.append(pl.BlockSpec((tm, QK_GLA_COLS), lambda i: (i, 0)))
        args += [walr, wa2, ba]
    res = pl.pallas_call(
        functools.partial(_prenorm_kernel, with_gate=gate is not None),
        grid=(m // tm,), in_specs=in_specs, out_specs=out_specs, out_shape=out_shape,
        compiler_params=_params("parallel"), name="prenorm_gate" if gate is not None else "prenorm",
    )(*args)
    return res if gate is not None else res[0]


def _proj_kernel(a_ref, w_ref, *rest, epi):
    o_ref = rest[-1]
    acc = _dot(a_ref[...], w_ref[...])
    if epi == "plain":
        o_ref[...] = acc
    elif epi == "silu":
        o_ref[...] = acc * jax.nn.sigmoid(acc)
    else:
        g_ref, cos_ref, sin_ref = rest[:3]
        o_ref[...] = acc
        cos = cos_ref[...]
        sin = sin_ref[...]
        for hh in range(o_ref.shape[1] // HEAD_DIM):
            sl = slice(hh * HEAD_DIM, (hh + 1) * HEAD_DIM)
            x = o_ref[:, sl]
            ms = jnp.mean(x * x, axis=-1, keepdims=True)
            y = x * lax.rsqrt(ms + EPS) * g_ref[:, sl]
            o_ref[:, sl] = y * cos + pltpu.roll(y, HEAD_DIM // 2, 1) * sin


def _proj(xn, w, sec, epi, aux=(), tm=1024):
    m, d = xn.shape
    tm = min(tm, m)
    blk0, nblk = sec
    tn = PROJ_TN
    in_specs = [pl.BlockSpec((tm, d), lambda i, j: (i, 0)),
                pl.BlockSpec((d, tn), lambda i, j: (0, blk0 + j))]
    if epi == "rope":
        in_specs += [pl.BlockSpec((1, tn), lambda i, j: (0, j)),
                     pl.BlockSpec((tm, HEAD_DIM), lambda i, j: (i, 0)),
                     pl.BlockSpec((tm, HEAD_DIM), lambda i, j: (i, 0))]
    return pl.pallas_call(
        functools.partial(_proj_kernel, epi=epi),
        grid=(m // tm, nblk), in_specs=in_specs,
        out_specs=pl.BlockSpec((tm, tn), lambda i, j: (i, j)),
        out_shape=jax.ShapeDtypeStruct((m, nblk * tn), F32),
        compiler_params=_params("parallel", "parallel"), name="proj_" + epi,
    )(xn, w, *aux)


def _gla_level_matrix(c):
    nl = int(math.log2(c))
    t = np.arange(c)[:, None]
    u = np.arange(c)[None, :]
    blocks = [u <= t, u > t]
    for lvl in range(nl):
        mid = ((t >> (lvl + 1)) << (lvl + 1)) + (1 << lvl) - 1
        right = ((t >> lvl) & 1) == 1
        blocks.append(np.where(right, (u > mid) & (u <= t), (u > t) & (u <= mid)))
    return np.concatenate(blocks, 0).astype(np.float32), nl


def _gla_kernel(q_ref, k_ref, v_ref, g_ref, r_ref, gn_ref, m_ref, *rest, c, nl, has_init):
    if has_init:
        s0_ref, o_ref, sout_ref, s_scr = rest
    else:
        o_ref, sout_ref, s_scr = rest
    ci = pl.program_id(1)

    @pl.when(ci == 0)
    def _():
        if has_init:
            s_scr[...] = s0_ref[0]
        else:
            s_scr[...] = jnp.zeros_like(s_scr)

    q = q_ref[...] * (DK_GLA ** -0.5)
    k = k_ref[...]
    vb = v_ref[...].astype(BF16)
    g = g_ref[...]
    g_hi = g.astype(BF16)
    r1 = g - g_hi.astype(F32)
    g_mid = r1.astype(BF16)
    g_lo = (r1 - g_mid.astype(F32)).astype(BF16)
    mm = m_ref[...]
    e = _dot(mm, g_hi) + _dot(mm, g_mid) + _dot(mm, g_lo)
    bcum = e[0:c]
    brest = e[c:2 * c]
    s = s_scr[...]

    o = _dot((q * jnp.exp(bcum)).astype(BF16), s.astype(BF16))

    row = lax.broadcasted_iota(jnp.int32, (c, DK_GLA), 0)
    ti = lax.broadcasted_iota(jnp.int32, (c, c), 0)
    si = lax.broadcasted_iota(jnp.int32, (c, c), 1)
    tx = ti ^ si
    att = jnp.where(ti == si, _dot_nt(q.astype(BF16), k.astype(BF16)), 0.0)
    for lvl in range(nl):
        el = e[(2 + lvl) * c:(3 + lvl) * c]
        right = ((row >> lvl) & 1) == 1
        x = (jnp.where(right, q, k) * jnp.exp(el)).astype(BF16)
        p = _dot_nt(x, x)
        att = att + jnp.where((ti > si) & ((tx >> lvl) == 1), p, 0.0)
    o = o + _dot(att.astype(BF16), vb)

    ones = jnp.ones((c, LANES), BF16)
    b_last = _dot_tn(g_hi, ones) + _dot_tn(g_mid, ones) + _dot_tn(g_lo, ones)
    dec = jnp.exp(b_last)
    dec = jnp.concatenate([dec] * (DV_GLA // LANES), axis=1)
    s_new = dec * s + _dot_tn((k * jnp.exp(brest)).astype(BF16), vb)
    s_scr[...] = s_new

    @pl.when(ci == pl.num_programs(1) - 1)
    def _():
        sout_ref[0] = s_new

    ms = jnp.mean(o * o, axis=-1, keepdims=True)
    o_ref[...] = (o * lax.rsqrt(ms + EPS) * gn_ref[...] * r_ref[...]).astype(BF16)


def _gla(q, k, v, g, r, gn, *, c, n_seq, n_chunks, maps, s0=None):
    mat, nl = _gla_level_matrix(c)
    mat = jnp.asarray(mat, BF16)
    qmap, kmap, vmap, gmap, rmap, omap = maps
    in_specs = [pl.BlockSpec((c, DK_GLA), qmap), pl.BlockSpec((c, DK_GLA), kmap),
                pl.BlockSpec((c, DV_GLA), vmap), pl.BlockSpec((c, DK_GLA), gmap),
                pl.BlockSpec((c, DV_GLA), rmap), pl.BlockSpec((1, DV_GLA), lambda s, ci: (0, 0)),
                pl.BlockSpec(mat.shape, lambda s, ci: (0, 0))]
    args = [q, k, v, g, r, gn.reshape(1, DV_GLA), mat]
    if s0 is not None:
        in_specs.append(pl.BlockSpec((1, DK_GLA, DV_GLA), lambda s, ci: (s, 0, 0)))
        args.append(s0)
    o_rows = r.shape[0]
    o_cols = r.shape[1]
    return pl.pallas_call(
        functools.partial(_gla_kernel, c=c, nl=nl, has_init=s0 is not None),
        grid=(n_seq, n_chunks), in_specs=in_specs,
        out_specs=[pl.BlockSpec((c, DV_GLA), omap),
                   pl.BlockSpec((1, DK_GLA, DV_GLA), lambda s, ci: (s, 0, 0))],
        out_shape=[jax.ShapeDtypeStruct((o_rows, o_cols), BF16),
                   jax.ShapeDtypeStruct((n_seq, DK_GLA, DV_GLA), F32)],
        scratch_shapes=[pltpu.VMEM((DK_GLA, DV_GLA), F32)],
        compiler_params=_params("parallel", "arbitrary"), name="gla_c%d" % c,
    )(*args)


DSWA_TQ = 2048
DSWA_SUB = 128


def _dswa_prompt_kernel(q_ref, kp_ref, kc_ref, vp_ref, vc_ref, o_ref,
                        a1, m1, l1, a2, m2, l2, a3, m3, l3):
    sub = DSWA_SUB
    tq = DSWA_TQ
    prev_bias = jnp.where(pl.program_id(1) > 0, 0.0, -jnp.inf).astype(F32)
    iq = lax.broadcasted_iota(jnp.int32, (sub, sub), 0)
    im = lax.broadcasted_iota(jnp.int32, (sub, sub), 1)
    mask_a = im >= iq
    mask_b = im <= iq

    def rows(ref, start, stride):
        if stride == 1:
            return ref[start:start + sub, :]
        return ref[pl.ds(start, sub, stride=stride), :]

    def solve(start, stride, a_from_prev, a_start, acc, mref, lref):
        qb = rows(q_ref, start, stride).astype(BF16)
        kb_ = rows(kc_ref, start, stride).astype(BF16)
        vb_ = rows(vc_ref, start, stride).astype(BF16)
        if a_from_prev:
            ka = rows(kp_ref, a_start, stride).astype(BF16)
            va = rows(vp_ref, a_start, stride).astype(BF16)
        else:
            ka = rows(kc_ref, a_start, stride).astype(BF16)
            va = rows(vc_ref, a_start, stride).astype(BF16)
        sa = _dot_nt(qb, ka) * DSW_SCALE
        if a_from_prev:
            sa = sa + prev_bias
        sa = jnp.where(mask_a, sa, -jnp.inf)
        sb = jnp.where(mask_b, _dot_nt(qb, kb_) * DSW_SCALE, -jnp.inf)
        m = jnp.maximum(jnp.max(sa, axis=-1, keepdims=True), jnp.max(sb, axis=-1, keepdims=True))
        pa = jnp.exp(sa - m)
        pb = jnp.exp(sb - m)
        l = jnp.sum(pa, axis=-1, keepdims=True) + jnp.sum(pb, axis=-1, keepdims=True)
        o = _dot(pa.astype(BF16), va) + _dot(pb.astype(BF16), vb_)
        if stride == 1:
            dst = slice(start, start + sub)
        else:
            dst = pl.ds(start, sub, stride=stride)
        acc[dst, :] = o
        mref[dst, :] = jnp.broadcast_to(m, (sub, LANES))
        lref[dst, :] = jnp.broadcast_to(l, (sub, LANES))

    for stride, win, acc, mref, lref in ((1, 128, a1, m1, l1), (4, 512, a2, m2, l2), (16, 2048, a3, m3, l3)):
        span = sub * stride
        for jb in range(tq // span):
            for res in range(stride):
                start = jb * span + res
                if jb == 0:
                    solve(start, stride, True, tq - win + res, acc, mref, lref)
                else:
                    solve(start, stride, False, (jb - 1) * span + res, acc, mref, lref)

    mx = jnp.maximum(jnp.maximum(m1[...], m2[...]), m3[...])
    w1 = jnp.exp(m1[...] - mx)
    w2 = jnp.exp(m2[...] - mx)
    w3 = jnp.exp(m3[...] - mx)
    num = w1 * a1[...] + w2 * a2[...] + w3 * a3[...]
    den = w1 * l1[...] + w2 * l2[...] + w3 * l3[...]
    o_ref[...] = (num / den).astype(o_ref.dtype)


def _dswa_prompt(qk, v):
    t = v.shape[0]
    tq = DSWA_TQ
    nh = N_HEADS_DSW
    blk = (tq, HEAD_DIM)
    prev = lambda h, i: jnp.maximum(i - 1, 0)
    in_specs = [pl.BlockSpec(blk, lambda h, i: (i, h)),
                pl.BlockSpec(blk, lambda h, i: (prev(h, i), nh + h)),
                pl.BlockSpec(blk, lambda h, i: (i, nh + h)),
                pl.BlockSpec(blk, lambda h, i: (prev(h, i), h)),
                pl.BlockSpec(blk, lambda h, i: (i, h))]
    return pl.pallas_call(
        _dswa_prompt_kernel, grid=(nh, t // tq), in_specs=in_specs,
        out_specs=pl.BlockSpec(blk, lambda h, i: (i, h)),
        out_shape=jax.ShapeDtypeStruct((t, DSW_COLS), BF16),
        scratch_shapes=[pltpu.VMEM(blk, F32)] * 9,
        compiler_params=_params("parallel", "arbitrary"), name="dswa_prompt",
    )(qk, qk, qk, v, v)


CACHE_GROUP = 16
SAMPLE_L = 4
N_GROUPS = W_MAX // CACHE_GROUP
FAR_GROUPS = 96
NEAR_GROUPS = N_GROUPS - FAR_GROUPS
NEW_ROWS = 16
SAMPLE_ROWS = SAMPLE_L * FAR_GROUPS + CACHE_GROUP * NEAR_GROUPS + NEW_ROWS
ROW_W = N_HEADS_DSW * HEAD_DIM


def _sample_counts():
    cnt = np.zeros((SAMPLE_ROWS, SAMPLE_L * N_HEADS_DSW), np.float32)
    qj = np.repeat(np.arange(SAMPLE_L), N_HEADS_DSW)[None, :]
    row = 0
    for r in range(SAMPLE_L):
        cnt[row:row + FAR_GROUPS] = (qj == r)
        row += FAR_GROUPS
    for r in range(CACHE_GROUP):
        gg = (FAR_GROUPS + np.arange(NEAR_GROUPS))[:, None]
        dist = W_MAX + qj - (CACHE_GROUP * gg + r)
        c = ((dist % 16 == 0) & (dist <= 2048)).astype(np.float32)
        c += ((dist % 4 == 0) & (dist <= 512))
        c += (dist <= 128)
        cnt[row:row + NEAR_GROUPS] = c
        row += NEAR_GROUPS
    n = np.arange(NEW_ROWS)[:, None]
    cnt[row:] = np.where(n < SAMPLE_L, (n <= qj) + 2.0 * (n == qj), 0.0)
    return cnt


def _dswa_sample_kernel(q_ref, kn_ref, vn_ref, kf_ref, kr_ref, vf_ref, vr_ref, cnt_ref, o_ref):
    nq = SAMPLE_L * N_HEADS_DSW
    q = q_ref[0]
    col_head = lax.broadcasted_iota(jnp.int32, (N_HEADS_DSW, ROW_W), 1) // HEAD_DIM
    row_head = lax.broadcasted_iota(jnp.int32, (N_HEADS_DSW, ROW_W), 0)
    diag = col_head == row_head
    qbd = jnp.concatenate(
        [jnp.where(diag, jnp.broadcast_to(q[j:j + 1, :], (N_HEADS_DSW, ROW_W)), 0.0) for j in range(SAMPLE_L)],
        axis=0).astype(BF16)

    def slabs(far_ref, near_ref, new_ref):
        out = [far_ref[0, :, r * ROW_W:(r + 1) * ROW_W].astype(BF16) for r in range(SAMPLE_L)]
        out += [near_ref[0, :, r * ROW_W:(r + 1) * ROW_W].astype(BF16) for r in range(CACHE_GROUP)]
        out.append(new_ref[0].astype(BF16))
        return out

    s = jnp.concatenate([_dot_nt(kk, qbd) for kk in slabs(kf_ref, kr_ref, kn_ref)], axis=0) * DSW_SCALE
    cnt = cnt_ref[...]
    s = jnp.where(cnt > 0.0, s, -jnp.inf)
    m = jnp.max(s, axis=0, keepdims=True)
    p = (cnt * jnp.exp(s - m)).astype(BF16)
    den = _dot_tn(p, jnp.ones((SAMPLE_ROWS, LANES), BF16))
    acc = jnp.zeros((nq, ROW_W), F32)
    row = 0
    for vv in slabs(vf_ref, vr_ref, vn_ref):
        n = vv.shape[0]
        acc = acc + _dot_tn(p[row:row + n], vv)
        row += n
    acc = acc * jnp.concatenate([1.0 / den] * N_HEADS_DSW, axis=1)
    sel_c = lax.broadcasted_iota(jnp.int32, (nq, ROW_W), 1) // HEAD_DIM
    sel_r = lax.broadcasted_iota(jnp.int32, (nq, ROW_W), 0) % N_HEADS_DSW
    acc = jnp.where(sel_c == sel_r, acc, 0.0)
    o_ref[0] = jnp.concatenate(
        [jnp.sum(acc[j * N_HEADS_DSW:(j + 1) * N_HEADS_DSW], axis=0, keepdims=True) for j in range(SAMPLE_L)],
        axis=0).astype(o_ref.dtype)


def _dswa_sample(q, k_new, v_new, cache_k, cache_v):
    b = q.shape[0]
    assert q.shape[1] == SAMPLE_L and cache_k.shape[1] == W_MAX
    pad = ((0, 0), (0, NEW_ROWS - SAMPLE_L), (0, 0))
    kn = jnp.pad(k_new, pad)
    vn = jnp.pad(v_new, pad)
    ck = cache_k.reshape(b, N_GROUPS, CACHE_GROUP * ROW_W)
    cv = cache_v.reshape(b, N_GROUPS, CACHE_GROUP * ROW_W)
    cnt = jnp.asarray(_sample_counts())
    far = pl.BlockSpec((1, FAR_GROUPS, SAMPLE_L * ROW_W), lambda i: (i, 0, 0))
    near = pl.BlockSpec((1, NEAR_GROUPS, CACHE_GROUP * ROW_W), lambda i: (i, FAR_GROUPS // NEAR_GROUPS, 0))
    small = pl.BlockSpec((1, SAMPLE_L, ROW_W), lambda i: (i, 0, 0))
    new = pl.BlockSpec((1, NEW_ROWS, ROW_W), lambda i: (i, 0, 0))
    return pl.pallas_call(
        _dswa_sample_kernel, grid=(b,),
        in_specs=[small, new, new, far, near, far, near, pl.BlockSpec(cnt.shape, lambda i: (0, 0))],
        out_specs=small, out_shape=jax.ShapeDtypeStruct((b, SAMPLE_L, ROW_W), F32),
        compiler_params=_params("parallel"), name="dswa_sample",
    )(q, kn, vn, ck, ck, cv, cv, cnt)


def _outproj_kernel(a1_ref, a2_ref, w1_ref, w2_ref, res_ref, o_ref):
    o_ref[...] = res_ref[...] + _dot(a1_ref[...], w1_ref[...]) + _dot(a2_ref[...], w2_ref[...])


def _outproj(a1, a2, w, res, tm=1024, tn=1024):
    m, kh = a1.shape
    n = w.shape[1]
    tm = min(tm, m)
    return pl.pallas_call(
        _outproj_kernel, grid=(m // tm, n // tn),
        in_specs=[pl.BlockSpec((tm, kh), lambda i, j: (i, 0)), pl.BlockSpec((tm, kh), lambda i, j: (i, 0)),
                  pl.BlockSpec((kh, tn), lambda i, j: (0, j)), pl.BlockSpec((kh, tn), lambda i, j: (1, j)),
                  pl.BlockSpec((tm, tn), lambda i, j: (i, j))],
        out_specs=pl.BlockSpec((tm, tn), lambda i, j: (i, j)),
        out_shape=jax.ShapeDtypeStruct((m, n), F32),
        compiler_params=_params("parallel", "parallel"), name="outproj",
    )(a1, a2, w, w, res)


def _swiglu_kernel(a_ref, wg_ref, wu_ref, o_ref):
    a = a_ref[...]
    gate = _dot(a, wg_ref[...])
    o_ref[...] = (gate * jax.nn.sigmoid(gate) * _dot(a, wu_ref[...])).astype(BF16)


def _swiglu(a, wg, wu, tm=1024, tn=256):
    m, d = a.shape
    n = wg.shape[1]
    tm = min(tm, m)
    return pl.pallas_call(
        _swiglu_kernel, grid=(m // tm, n // tn),
        in_specs=[pl.BlockSpec((tm, d), lambda i, j: (i, 0)),
                  pl.BlockSpec((d, tn), lambda i, j: (0, j)), pl.BlockSpec((d, tn), lambda i, j: (0, j))],
        out_specs=pl.BlockSpec((tm, tn), lambda i, j: (i, j)),
        out_shape=jax.ShapeDtypeStruct((m, n), BF16),
        compiler_params=_params("parallel", "parallel"), name="swiglu",
    )(a, wg, wu)


def _down_kernel(a_ref, w_ref, res_ref, o_ref):
    part = _dot(a_ref[...], w_ref[...])

    @pl.when(pl.program_id(2) == 0)
    def _():
        o_ref[...] = res_ref[...] + part

    @pl.when(pl.program_id(2) > 0)
    def _():
        o_ref[...] += part


def _down(a, w, res, tm=1024, tn=512, ksplit=2):
    m, kk = a.shape
    n = w.shape[1]
    tm = min(tm, m)
    tk = kk // ksplit
    assert tk * ksplit == kk and tk % LANES == 0
    return pl.pallas_call(
        _down_kernel, grid=(m // tm, n // tn, ksplit),
        in_specs=[pl.BlockSpec((tm, tk), lambda i, j, k: (i, k)), pl.BlockSpec((tk, tn), lambda i, j, k: (k, j)),
                  pl.BlockSpec((tm, tn), lambda i, j, k: (i, j))],
        out_specs=pl.BlockSpec((tm, tn), lambda i, j, k: (i, j)),
        out_shape=jax.ShapeDtypeStruct((m, n), F32),
        compiler_params=_params("parallel", "parallel", "arbitrary"), name="down",
    )(a, w, res)


def _rope_tables(pos):
    half = HEAD_DIM // 2
    inv = ROPE_THETA ** (-jnp.arange(half, dtype=F32) / half)
    ang = pos.astype(F32)[:, None] * inv[None, :]
    cos = jnp.cos(ang)
    sin = jnp.sin(ang)
    return jnp.concatenate([cos, cos], axis=1), jnp.concatenate([-sin, sin], axis=1)


def _layer_weights(w_in, w_a2, b_a, qn_g, kn_g, w_out, w_gate, w_up, w_down):
    w_main = jnp.concatenate([w_in[:, :ALR_OFF], w_in[:, ALR_OFF + GLA_LOWRANK:]], axis=1).astype(BF16)
    w_alr = jnp.pad(w_in[:, ALR_OFF:ALR_OFF + GLA_LOWRANK], ((0, 0), (0, LANES - GLA_LOWRANK))).astype(BF16)
    w_a2p = jnp.pad(w_a2, ((0, LANES - GLA_LOWRANK), (0, 0)))
    qk_gain = jnp.concatenate([jnp.tile(qn_g, N_HEADS_DSW), jnp.tile(kn_g, N_HEADS_DSW)]).reshape(1, 2 * DSW_COLS)
    return dict(w_main=w_main, gate=(w_alr, w_a2p, b_a.reshape(1, QK_GLA_COLS)), qk_gain=qk_gain,
                w_out=w_out.astype(BF16), w_gate=w_gate.astype(BF16), w_up=w_up.astype(BF16),
                w_down=w_down.astype(BF16))


def _project_all(x2d, norm1_g, lw, pos):
    xn, log_a = _prenorm(x2d, norm1_g, gate=lw["gate"])
    cos, sin = _rope_tables(pos)
    qkv_a = _proj(xn, lw["w_main"], SEC_QKV_GLA, "plain")
    r_a = _proj(xn, lw["w_main"], SEC_R_GLA, "silu")
    qk_d = _proj(xn, lw["w_main"], SEC_QK_DSW, "rope", aux=(lw["qk_gain"], cos, sin))
    v_d = _proj(xn, lw["w_main"], SEC_V_DSW, "plain")
    return log_a, qkv_a, r_a, qk_d, v_d


def _finish(x2d, o_a, o_d, norm2_g, lw):
    h = _outproj(o_a, o_d, lw["w_out"], x2d)
    hn = _prenorm(h, norm2_g)
    u = _swiglu(hn, lw["w_gate"], lw["w_up"])
    return _down(u, lw["w_down"], h)


GLA_CHUNK_PROMPT = 128
GLA_CHUNK_SAMPLE = 16


def _prompt_layer(x2d, norm1_g, gla_norm_g, norm2_g, lw):
    t = x2d.shape[0]
    log_a, qkv_a, r_a, qk_d, v_d = _project_all(x2d, norm1_g, lw, jnp.arange(t))
    c = GLA_CHUNK_PROMPT
    nh = N_HEADS_GLA
    maps = (lambda h, ci: (ci, h), lambda h, ci: (ci, nh + h), lambda h, ci: (ci, nh + h),
            lambda h, ci: (ci, h), lambda h, ci: (ci, h), lambda h, ci: (ci, h))
    o_a, s_fin = _gla(qkv_a, qkv_a, qkv_a, log_a, r_a, gla_norm_g, c=c, n_seq=nh, n_chunks=t // c, maps=maps)
    o_d = _dswa_prompt(qk_d, v_d)
    y = _finish(x2d, o_a, o_d, norm2_g, lw)
    lwp = min(W_MAX, t)
    k_tail = qk_d[t - lwp:, DSW_COLS:].reshape(1, lwp, N_HEADS_DSW, HEAD_DIM)
    v_tail = v_d[t - lwp:].reshape(1, lwp, N_HEADS_DSW, HEAD_DIM)
    return y, k_tail, v_tail, s_fin[None]


def _sample_layer(x2d, nb, ls, past, cache_k, cache_v, state, norm1_g, gla_norm_g, norm2_g, lw):
    pos = past + jnp.tile(jnp.arange(ls), nb)
    log_a, qkv_a, r_a, qk_d, v_d = _project_all(x2d, norm1_g, lw, pos)
    c = GLA_CHUNK_SAMPLE
    nh = N_HEADS_GLA

    def seqs(a, width):
        a = a.reshape(nb, ls, nh, width).transpose(0, 2, 1, 3)
        a = jnp.pad(a, ((0, 0), (0, 0), (0, c - ls), (0, 0)))
        return a.reshape(nb * nh * c, width)

    q = seqs(qkv_a[:, :QK_GLA_COLS], DK_GLA)
    k = seqs(qkv_a[:, QK_GLA_COLS:2 * QK_GLA_COLS], DK_GLA)
    v = seqs(qkv_a[:, 2 * QK_GLA_COLS:], DV_GLA)
    g = seqs(log_a, DK_GLA)
    r = seqs(r_a, DV_GLA)
    blk = lambda s, ci: (s, 0)
    o_pad, s_fin = _gla(q, k, v, g, r, gla_norm_g, c=c, n_seq=nb * nh, n_chunks=1, maps=(blk,) * 6,
                        s0=state.reshape(nb * nh, DK_GLA, DV_GLA))
    o_a = o_pad.reshape(nb, nh, c, DV_GLA)[:, :, :ls].transpose(0, 2, 1, 3).reshape(nb * ls, V_GLA_COLS)
    q_d = qk_d[:, :DSW_COLS].reshape(nb, ls, DSW_COLS)
    k_d = qk_d[:, DSW_COLS:].reshape(nb, ls, DSW_COLS)
    o_d = _dswa_sample(q_d, k_d, v_d.reshape(nb, ls, DSW_COLS), cache_k, cache_v)
    o_d = o_d.reshape(nb * ls, DSW_COLS).astype(BF16)
    y = _finish(x2d, o_a, o_d, norm2_g, lw)
    k_new = k_d.reshape(nb, ls, N_HEADS_DSW, HEAD_DIM)
    v_new = v_d.reshape(nb, ls, N_HEADS_DSW, HEAD_DIM)
    return y, k_new, v_new, s_fin.reshape(nb, nh, DK_GLA, DV_GLA)


def kernel(x_prompt, x_sample, cache_k, cache_v, state_gla, norm1_g, w_in, w_a2, b_a, qn_g, kn_g,
           gla_norm_g, w_out, norm2_g, w_gate, w_up, w_down):
    bp, tp, d = x_prompt.shape
    nb, ls, _ = x_sample.shape
    assert bp == 1 and tp % DSWA_TQ == 0 and cache_k.shape[2] == W_MAX
    past = PAST_LEN
    yp = x_prompt.reshape(tp, d)
    ys = x_sample.reshape(nb * ls, d)
    outs = [[] for _ in range(6)]
    for li in range(w_in.shape[0]):
        lw = _layer_weights(w_in[li], w_a2[li], b_a[li], qn_g[li], kn_g[li], w_out[li], w_gate[li],
                            w_up[li], w_down[li])
        yp, kp, vp, sp = _prompt_layer(yp, norm1_g[li], gla_norm_g[li], norm2_g[li], lw)
        ys, ks, vs, ss = _sample_layer(ys, nb, ls, past, cache_k[li], cache_v[li], state_gla[li],
                                       norm1_g[li], gla_norm_g[li], norm2_g[li], lw)
        for lst, val in zip(outs, (kp, vp, sp, ks, vs, ss)):
            lst.append(val)
    stacked = [jnp.stack(lst, axis=0) for lst in outs]
    return (yp.reshape(bp, tp, d), ys.reshape(nb, ls, d), *stacked)
```

```python
import functools
import math

import numpy as np
import jax
import jax.numpy as jnp
from jax import lax
from jax.experimental import pallas as pl
from jax.experimental.pallas import tpu as pltpu

F32 = jnp.float32
BF16 = jnp.bfloat16

D_MODEL = 4096
HEAD_DIM = 128
N_HEADS_GLA = 4
DV_GLA = 512
DK_GLA = 256
GLA_LOWRANK = 16
GLA_TAU = 16.0
N_HEADS_DSW = 16
W_MAX = 2048
PAST_LEN = 2048
D_FF = 11008
ROPE_THETA = 10000.0
EPS = 1e-6
QK_GLA_COLS = N_HEADS_GLA * DK_GLA
V_GLA_COLS = N_HEADS_GLA * DV_GLA
DSW_COLS = N_HEADS_DSW * HEAD_DIM
ALR_OFF = 2 * QK_GLA_COLS + V_GLA_COLS
DSW_SCALE = HEAD_DIM ** -0.5

LANES = 128
VMEM_LIMIT = 56 * 1024 * 1024

SEC_QKV_GLA = (0, 4)
SEC_R_GLA = (4, 2)
SEC_QK_DSW = (6, 4)
SEC_V_DSW = (10, 2)
PROJ_TN = 1024


def _dot(a, b):
    return jnp.dot(a, b, preferred_element_type=F32)


def _dot_nt(a, b):
    return lax.dot_general(a, b, (((1,), (1,)), ((), ())), preferred_element_type=F32)


def _dot_tn(a, b):
    return lax.dot_general(a, b, (((0,), (0,)), ((), ())), preferred_element_type=F32)


def _params(*sem):
    return pltpu.CompilerParams(dimension_semantics=sem, vmem_limit_bytes=VMEM_LIMIT)


def _prenorm_kernel(x_ref, g_ref, *rest, with_gate):
    x = x_ref[...]
    ms = jnp.mean(x * x, axis=-1, keepdims=True)
    xn = (x * lax.rsqrt(ms + EPS) * g_ref[...]).astype(BF16)
    if not with_gate:
        (xn_ref,) = rest
        xn_ref[...] = xn
        return
    walr_ref, wa2_ref, ba_ref, xn_ref, la_ref = rest
    xn_ref[...] = xn
    a = _dot(xn, walr_ref[...])
    z = jnp.dot(a, wa2_ref[...], precision=lax.Precision.HIGHEST,
                preferred_element_type=F32) + ba_ref[...]
    la_ref[...] = (jnp.minimum(z, 0.0) - jnp.log1p(jnp.exp(-jnp.abs(z)))) * (1.0 / GLA_TAU)


def _prenorm(x, g, gate=None, tm=256):
    m, d = x.shape
    tm = min(tm, m)
    in_specs = [pl.BlockSpec((tm, d), lambda i: (i, 0)), pl.BlockSpec((1, d), lambda i: (0, 0))]
    out_shape = [jax.ShapeDtypeStruct((m, d), BF16)]
    out_specs = [pl.BlockSpec((tm, d), lambda i: (i, 0))]
    args = [x, g.reshape(1, d)]
    if gate is not None:
        walr, wa2, ba = gate
        in_specs += [pl.BlockSpec((d, LANES), lambda i: (0, 0)),
                     pl.BlockSpec((LANES, QK_GLA_COLS), lambda i: (0, 0)),
                     pl.BlockSpec((1, QK_GLA_COLS), lambda i: (0, 0))]
        out_shape.append(jax.ShapeDtypeStruct((m, QK_GLA_COLS), F32))
        out_specs.append(pl.BlockSpec((tm, QK_GLA_COLS), lambda i: (i, 0)))
        args += [walr, wa2, ba]
    res = pl.pallas_call(
        functools.partial(_prenorm_kernel, with_gate=gate is not None),
        grid=(m // tm,), in_specs=in_specs, out_specs=out_specs, out_shape=out_shape,
        compiler_params=_params("parallel"), name="prenorm_gate" if gate is not None else "prenorm",
    )(*args)
    return res if gate is not None else res[0]


def _proj_kernel(a_ref, w_ref, *rest, epi):
    o_ref = rest[-1]
    acc = _dot(a_ref[...], w_ref[...])
    if epi == "plain":
        o_ref[...] = acc
    elif epi == "silu":
        o_ref[...] = acc * jax.nn.sigmoid(acc)
    else:
        g_ref, cos_ref, sin_ref = rest[:3]
        o_ref[...] = acc
        cos = cos_ref[...]
        sin = sin_ref[...]
        for hh in range(o_ref.shape[1] // HEAD_DIM):
            sl = slice(hh * HEAD_DIM, (hh + 1) * HEAD_DIM)
            x = o_ref[:, sl]
            ms = jnp.mean(x * x, axis=-1, keepdims=True)
            y = x * lax.rsqrt(ms + EPS) * g_ref[:, sl]
            o_ref[:, sl] = y * cos + pltpu.roll(y, HEAD_DIM // 2, 1) * sin


def _proj(xn, w, sec, epi, aux=(), tm=1024):
    m, d = xn.shape
    tm = min(tm, m)
    blk0, nblk = sec
    tn = PROJ_TN
    in_specs = [pl.BlockSpec((tm, d), lambda i, j: (i, 0)),
                pl.BlockSpec((d, tn), lambda i, j: (0, blk0 + j))]
    if epi == "rope":
        in_specs += [pl.BlockSpec((1, tn), lambda i, j: (0, j)),
                     pl.BlockSpec((tm, HEAD_DIM), lambda i, j: (i, 0)),
                     pl.BlockSpec((tm, HEAD_DIM), lambda i, j: (i, 0))]
    return pl.pallas_call(
        functools.partial(_proj_kernel, epi=epi),
        grid=(m // tm, nblk), in_specs=in_specs,
        out_specs=pl.BlockSpec((tm, tn), lambda i, j: (i, j)),
        out_shape=jax.ShapeDtypeStruct((m, nblk * tn), F32),
        compiler_params=_params("parallel", "parallel"), name="proj_" + epi,
    )(xn, w, *aux)


def _gla_level_matrix(c):
    nl = int(math.log2(c))
    t = np.arange(c)[:, None]
    u = np.arange(c)[None, :]
    blocks = [u <= t, u > t]
    for lvl in range(nl):
        mid = ((t >> (lvl + 1)) << (lvl + 1)) + (1 << lvl) - 1
        right = ((t >> lvl) & 1) == 1
        blocks.append(np.where(right, (u > mid) & (u <= t), (u > t) & (u <= mid)))
    return np.concatenate(blocks, 0).astype(np.float32), nl


def _gla_kernel(q_ref, k_ref, v_ref, g_ref, r_ref, gn_ref, m_ref, *rest, c, nl, ns, by_cols, has_init, single):
    if has_init:
        s0_ref, o_ref, sout_ref, s_scr = rest
    else:
        o_ref, sout_ref, s_scr = rest
    ci = pl.program_id(1)

    if not single:
        @pl.when(ci == 0)
        def _():
            if has_init:
                s_scr[...] = s0_ref[...]
            else:
                s_scr[...] = jnp.zeros_like(s_scr)

    def slab(ref, i, width):
        return ref[:, i * width:(i + 1) * width] if by_cols else ref[i * c:(i + 1) * c, :]

    mm = m_ref[...]
    gn = gn_ref[...]
    row = lax.broadcasted_iota(jnp.int32, (c, DK_GLA), 0)
    ti = lax.broadcasted_iota(jnp.int32, (c, c), 0)
    si = lax.broadcasted_iota(jnp.int32, (c, c), 1)
    tx = ti ^ si
    ones = jnp.ones((c, LANES), BF16)

    loaded = []
    for i in range(ns):
        if single:
            s = s0_ref[i] if has_init else jnp.zeros((DK_GLA, DV_GLA), F32)
        else:
            s = s_scr[i]
        loaded.append((slab(q_ref, i, DK_GLA), slab(k_ref, i, DK_GLA), slab(v_ref, i, DV_GLA),
                       slab(g_ref, i, DK_GLA), slab(r_ref, i, DV_GLA), s))

    st = []
    for q, k, v, g, r, s in loaded:
        g_hi = g.astype(BF16)
        r1 = g - g_hi.astype(F32)
        g_mid = r1.astype(BF16)
        g_lo = (r1 - g_mid.astype(F32)).astype(BF16)
        st.append(dict(q=q * (DK_GLA ** -0.5), k=k, vb=v.astype(BF16), r=r, s=s, g=(g_hi, g_mid, g_lo)))
    for d in st:
        d["e"] = sum(_dot(mm, gp) for gp in d["g"])
    for d in st:
        d["o"] = _dot((d["q"] * jnp.exp(d["e"][0:c])).astype(BF16), d["s"].astype(BF16))
        d["att"] = jnp.where(ti == si, _dot_nt(d["q"].astype(BF16), d["k"].astype(BF16)), 0.0)
    for lvl in range(nl):
        right = ((row >> lvl) & 1) == 1
        pair = (ti > si) & ((tx >> lvl) == 1)
        for d in st:
            x = (jnp.where(right, d["q"], d["k"]) * jnp.exp(d["e"][(2 + lvl) * c:(3 + lvl) * c])).astype(BF16)
            d["att"] = d["att"] + jnp.where(pair, _dot_nt(x, x), 0.0)
    for d in st:
        d["o"] = d["o"] + _dot(d["att"].astype(BF16), d["vb"])
    results = []
    for d in st:
        b_last = sum(_dot_tn(gp, ones) for gp in d["g"])
        dec = jnp.concatenate([jnp.exp(b_last)] * (DV_GLA // LANES), axis=1)
        s_new = dec * d["s"] + _dot_tn((d["k"] * jnp.exp(d["e"][c:2 * c])).astype(BF16), d["vb"])
        o = d["o"]
        ms = jnp.mean(o * o, axis=-1, keepdims=True)
        results.append(((o * lax.rsqrt(ms + EPS) * gn * d["r"]).astype(BF16), s_new))

    for i, (o, s_new) in enumerate(results):
        if by_cols:
            o_ref[:, i * DV_GLA:(i + 1) * DV_GLA] = o
        else:
            o_ref[i * c:(i + 1) * c, :] = o
        if single:
            sout_ref[i] = s_new
        else:
            s_scr[i] = s_new

    if not single:
        @pl.when(ci == pl.num_programs(1) - 1)
        def _():
            for i, (_, s_new) in enumerate(results):
                sout_ref[i] = s_new


def _gla(q, k, v, g, r, gn, *, c, ns, n_groups, n_chunks, by_cols, maps, s0=None, s0_map=None):
    mat, nl = _gla_level_matrix(c)
    mat = jnp.asarray(mat, BF16)
    qmap, kmap, vmap, gmap, rmap, omap = maps
    kshape = (c, ns * DK_GLA) if by_cols else (ns * c, DK_GLA)
    vshape = (c, ns * DV_GLA) if by_cols else (ns * c, DV_GLA)
    smap = lambda s, ci: (s, 0, 0)
    in_specs = [pl.BlockSpec(kshape, qmap), pl.BlockSpec(kshape, kmap), pl.BlockSpec(vshape, vmap),
                pl.BlockSpec(kshape, gmap), pl.BlockSpec(vshape, rmap),
                pl.BlockSpec((1, DV_GLA), lambda s, ci: (0, 0)), pl.BlockSpec(mat.shape, lambda s, ci: (0, 0))]
    args = [q, k, v, g, r, gn.reshape(1, DV_GLA), mat]
    if s0 is not None:
        in_specs.append(pl.BlockSpec((ns, DK_GLA, DV_GLA), s0_map or smap))
        args.append(s0)
    return pl.pallas_call(
        functools.partial(_gla_kernel, c=c, nl=nl, ns=ns, by_cols=by_cols, has_init=s0 is not None,
                          single=n_chunks == 1),
        grid=(n_groups, n_chunks), in_specs=in_specs,
        out_specs=[pl.BlockSpec(vshape, omap), pl.BlockSpec((ns, DK_GLA, DV_GLA), smap)],
        out_shape=[jax.ShapeDtypeStruct(r.shape, BF16),
                   jax.ShapeDtypeStruct((n_groups * ns, DK_GLA, DV_GLA), F32)],
        scratch_shapes=[pltpu.VMEM((ns, DK_GLA, DV_GLA), F32)],
        compiler_params=_params("parallel", "arbitrary"), name="gla_c%d" % c,
    )(*args)


DSWA_TQ = 2048
DSWA_SUB = 128
DSWA_GROUP = 4


def _dswa_prompt_kernel(q_ref, kp_ref, kc_ref, vp_ref, vc_ref, o_ref, a1, e1, a2, e2, a3, e3):
    sub = DSWA_SUB
    tq = DSWA_TQ
    prev_bias = jnp.where(pl.program_id(1) > 0, 0.0, -jnp.inf).astype(F32)
    iq = lax.broadcasted_iota(jnp.int32, (sub, sub), 0)
    im = lax.broadcasted_iota(jnp.int32, (sub, sub), 1)
    mask_a = im >= iq
    mask_b = im <= iq

    def rows(ref, start, stride):
        if stride == 1:
            return ref[start:start + sub, :].astype(BF16)
        return ref[pl.ds(start, sub, stride=stride), :].astype(BF16)

    def solve_group(items, stride, acc, lse_ref):
        loaded = []
        for start, a_from_prev, a_start in items:
            kref, vref = (kp_ref, vp_ref) if a_from_prev else (kc_ref, vc_ref)
            loaded.append((rows(q_ref, start, stride), rows(kref, a_start, stride), rows(kc_ref, start, stride),
                           rows(vref, a_start, stride), rows(vc_ref, start, stride)))
        scores = [(_dot_nt(qb, ka), _dot_nt(qb, kb_)) for qb, ka, kb_, _, _ in loaded]
        probs = []
        for (start, a_from_prev, a_start), (sa, sb) in zip(items, scores):
            sa = sa * DSW_SCALE
            if a_from_prev:
                sa = sa + prev_bias
            sa = jnp.where(mask_a, sa, -jnp.inf)
            sb = jnp.where(mask_b, sb * DSW_SCALE, -jnp.inf)
            m = jnp.maximum(jnp.max(sa, axis=-1, keepdims=True), jnp.max(sb, axis=-1, keepdims=True))
            pa = jnp.exp(sa - m)
            pb = jnp.exp(sb - m)
            l = jnp.sum(pa, axis=-1, keepdims=True) + jnp.sum(pb, axis=-1, keepdims=True)
            probs.append((pa.astype(BF16), pb.astype(BF16), 1.0 / l, m + jnp.log(l)))
        results = []
        for (pa, pb, inv_l, lse), (_, _, _, va, vb_) in zip(probs, loaded):
            o = (_dot(pa, va) + _dot(pb, vb_)) * inv_l
            results.append((o, jnp.broadcast_to(lse, (sub, LANES))))
        for (start, _, _), (o, lse) in zip(items, results):
            dst = slice(start, start + sub) if stride == 1 else pl.ds(start, sub, stride=stride)
            acc[dst, :] = o
            lse_ref[dst, :] = lse

    for stride, win, acc, lse_ref in ((1, 128, a1, e1), (4, 512, a2, e2), (16, 2048, a3, e3)):
        span = sub * stride
        items = []
        for jb in range(tq // span):
            for res in range(stride):
                start = jb * span + res
                items.append((start, True, tq - win + res) if jb == 0 else (start, False, (jb - 1) * span + res))
        for g0 in range(0, len(items), DSWA_GROUP):
            solve_group(items[g0:g0 + DSWA_GROUP], stride, acc, lse_ref)

    mx = jnp.maximum(jnp.maximum(e1[...], e2[...]), e3[...])
    w1 = jnp.exp(e1[...] - mx)
    w2 = jnp.exp(e2[...] - mx)
    w3 = jnp.exp(e3[...] - mx)
    o_ref[...] = ((w1 * a1[...] + w2 * a2[...] + w3 * a3[...]) / (w1 + w2 + w3)).astype(o_ref.dtype)


def _dswa_prompt(qk, v):
    t = v.shape[0]
    tq = DSWA_TQ
    nh = N_HEADS_DSW
    blk = (tq, HEAD_DIM)
    prev = lambda h, i: jnp.maximum(i - 1, 0)
    in_specs = [pl.BlockSpec(blk, lambda h, i: (i, h)),
                pl.BlockSpec(blk, lambda h, i: (prev(h, i), nh + h)),
                pl.BlockSpec(blk, lambda h, i: (i, nh + h)),
                pl.BlockSpec(blk, lambda h, i: (prev(h, i), h)),
                pl.BlockSpec(blk, lambda h, i: (i, h))]
    return pl.pallas_call(
        _dswa_prompt_kernel, grid=(nh, t // tq), in_specs=in_specs,
        out_specs=pl.BlockSpec(blk, lambda h, i: (i, h)),
        out_shape=jax.ShapeDtypeStruct((t, DSW_COLS), BF16),
        scratch_shapes=[pltpu.VMEM(blk, F32)] * 6,
        compiler_params=_params("parallel", "arbitrary"), name="dswa_prompt",
    )(qk, qk, qk, v, v)


CACHE_GROUP = 16
SAMPLE_L = 4
N_GROUPS = W_MAX // CACHE_GROUP
FAR_GROUPS = 96
NEAR_GROUPS = N_GROUPS - FAR_GROUPS
GROUP_ROWS = CACHE_GROUP * N_HEADS_DSW
FAR_ROWS = SAMPLE_L * N_HEADS_DSW
N_FAR = FAR_GROUPS * FAR_ROWS
N_NEAR = NEAR_GROUPS * GROUP_ROWS
N_NEW = LANES
SAMPLE_KEYS = N_FAR + N_NEAR + N_NEW


def _sample_counts():
    nq = SAMPLE_L * N_HEADS_DSW
    qj = (np.arange(nq) // N_HEADS_DSW)[:, None]
    qh = (np.arange(nq) % N_HEADS_DSW)[:, None]

    def branch_count(rho):
        dist = W_MAX + qj - rho
        c = ((dist % 16 == 0) & (dist <= 2048)).astype(np.float32)
        c += ((dist % 4 == 0) & (dist <= 512))
        c += (dist <= 128)
        return c

    col = np.arange(N_FAR)[None, :]
    g, rr, hh = col // FAR_ROWS, (col % FAR_ROWS) // N_HEADS_DSW, col % N_HEADS_DSW
    far = branch_count(CACHE_GROUP * g + rr) * (hh == qh)
    col = np.arange(N_NEAR)[None, :]
    g, rr, hh = FAR_GROUPS + col // GROUP_ROWS, (col % GROUP_ROWS) // N_HEADS_DSW, col % N_HEADS_DSW
    near = branch_count(CACHE_GROUP * g + rr) * (hh == qh)
    col = np.arange(N_NEW)[None, :]
    n, hh = col // N_HEADS_DSW, col % N_HEADS_DSW
    new = np.where(n < SAMPLE_L, (n <= qj) + 2.0 * (n == qj), 0.0) * (hh == qh)
    return np.concatenate([far, near, new], axis=1).astype(np.float32)


def _dswa_sample_kernel(q_ref, kn_ref, vn_ref, kf_ref, kr_ref, vf_ref, vr_ref, cnt_ref, o_ref):
    qb = q_ref[0].astype(BF16)
    kf = kf_ref[0].reshape(N_FAR, HEAD_DIM).astype(BF16)
    kr = kr_ref[0].reshape(N_NEAR, HEAD_DIM).astype(BF16)
    s = jnp.concatenate([_dot_nt(qb, kf), _dot_nt(qb, kr), _dot_nt(qb, kn_ref[0].astype(BF16))], axis=1)
    cnt = cnt_ref[...]
    s = jnp.where(cnt > 0.0, s * DSW_SCALE, -jnp.inf)
    m = jnp.max(s, axis=1, keepdims=True)
    p = cnt * jnp.exp(s - m)
    l = jnp.sum(p, axis=1, keepdims=True)
    pb = p.astype(BF16)
    acc = _dot(pb[:, :N_FAR], vf_ref[0].reshape(N_FAR, HEAD_DIM).astype(BF16))
    acc = acc + _dot(pb[:, N_FAR:N_FAR + N_NEAR], vr_ref[0].reshape(N_NEAR, HEAD_DIM).astype(BF16))
    acc = acc + _dot(pb[:, N_FAR + N_NEAR:], vn_ref[0].astype(BF16))
    o_ref[0] = acc * (1.0 / l)


def _dswa_sample(q, k_new, v_new, cache_k, cache_v, layer):
    b = q.shape[0]
    assert q.shape[1] == SAMPLE_L and cache_k.shape[2] == W_MAX
    nq = SAMPLE_L * N_HEADS_DSW
    rows = lambda a: jnp.pad(a.reshape(b, nq, HEAD_DIM), ((0, 0), (0, N_NEW - nq), (0, 0)))
    ck = cache_k.reshape(-1, N_GROUPS, GROUP_ROWS, HEAD_DIM)
    cv = cache_v.reshape(-1, N_GROUPS, GROUP_ROWS, HEAD_DIM)
    cnt = jnp.asarray(_sample_counts())
    off = layer * b
    far = pl.BlockSpec((1, FAR_GROUPS, FAR_ROWS, HEAD_DIM), lambda i: (off + i, 0, 0, 0))
    near = pl.BlockSpec((1, NEAR_GROUPS, GROUP_ROWS, HEAD_DIM), lambda i: (off + i, FAR_GROUPS // NEAR_GROUPS, 0, 0))
    qspec = pl.BlockSpec((1, nq, HEAD_DIM), lambda i: (i, 0, 0))
    new = pl.BlockSpec((1, N_NEW, HEAD_DIM), lambda i: (i, 0, 0))
    out = pl.pallas_call(
        _dswa_sample_kernel, grid=(b,),
        in_specs=[qspec, new, new, far, near, far, near, pl.BlockSpec(cnt.shape, lambda i: (0, 0))],
        out_specs=qspec, out_shape=jax.ShapeDtypeStruct((b, nq, HEAD_DIM), F32),
        compiler_params=_params("parallel"), name="dswa_sample",
    )(q.reshape(b, nq, HEAD_DIM), rows(k_new), rows(v_new), ck, ck, cv, cv, cnt)
    return out.reshape(b, SAMPLE_L, DSW_COLS)


def _outproj_kernel(a1_ref, a2_ref, w1_ref, w2_ref, res_ref, o_ref):
    o_ref[...] = res_ref[...] + _dot(a1_ref[...], w1_ref[...]) + _dot(a2_ref[...], w2_ref[...])


def _outproj(a1, a2, w, res, tm=1024, tn=1024):
    m, kh = a1.shape
    n = w.shape[1]
    tm = min(tm, m)
    return pl.pallas_call(
        _outproj_kernel, grid=(m // tm, n // tn),
        in_specs=[pl.BlockSpec((tm, kh), lambda i, j: (i, 0)), pl.BlockSpec((tm, kh), lambda i, j: (i, 0)),
                  pl.BlockSpec((kh, tn), lambda i, j: (0, j)), pl.BlockSpec((kh, tn), lambda i, j: (1, j)),
                  pl.BlockSpec((tm, tn), lambda i, j: (i, j))],
        out_specs=pl.BlockSpec((tm, tn), lambda i, j: (i, j)),
        out_shape=jax.ShapeDtypeStruct((m, n), F32),
        compiler_params=_params("parallel", "parallel"), name="outproj",
    )(a1, a2, w, w, res)


def _swiglu_kernel(a_ref, wg_ref, wu_ref, o_ref):
    a = a_ref[...]
    gate = _dot(a, wg_ref[...])
    o_ref[...] = (gate * jax.nn.sigmoid(gate) * _dot(a, wu_ref[...])).astype(BF16)


def _swiglu(a, wg, wu, tm=1024, tn=256):
    m, d = a.shape
    n = wg.shape[1]
    tm = min(tm, m)
    return pl.pallas_call(
        _swiglu_kernel, grid=(m // tm, n // tn),
        in_specs=[pl.BlockSpec((tm, d), lambda i, j: (i, 0)),
                  pl.BlockSpec((d, tn), lambda i, j: (0, j)), pl.BlockSpec((d, tn), lambda i, j: (0, j))],
        out_specs=pl.BlockSpec((tm, tn), lambda i, j: (i, j)),
        out_shape=jax.ShapeDtypeStruct((m, n), BF16),
        compiler_params=_params("parallel", "parallel"), name="swiglu",
    )(a, wg, wu)


def _down_kernel(a_ref, w_ref, res_ref, o_ref):
    part = _dot(a_ref[...], w_ref[...])

    @pl.when(pl.program_id(2) == 0)
    def _():
        o_ref[...] = res_ref[...] + part

    @pl.when(pl.program_id(2) > 0)
    def _():
        o_ref[...] += part


def _down(a, w, res, tm=1024, tn=512, ksplit=2):
    m, kk = a.shape
    n = w.shape[1]
    tm = min(tm, m)
    tk = kk // ksplit
    assert tk * ksplit == kk and tk % LANES == 0
    return pl.pallas_call(
        _down_kernel, grid=(m // tm, n // tn, ksplit),
        in_specs=[pl.BlockSpec((tm, tk), lambda i, j, k: (i, k)), pl.BlockSpec((tk, tn), lambda i, j, k: (k, j)),
                  pl.BlockSpec((tm, tn), lambda i, j, k: (i, j))],
        out_specs=pl.BlockSpec((tm, tn), lambda i, j, k: (i, j)),
        out_shape=jax.ShapeDtypeStruct((m, n), F32),
        compiler_params=_params("parallel", "parallel", "arbitrary"), name="down",
    )(a, w, res)


def _rope_tables(pos):
    half = HEAD_DIM // 2
    inv = ROPE_THETA ** (-jnp.arange(half, dtype=F32) / half)
    ang = pos.astype(F32)[:, None] * inv[None, :]
    cos = jnp.cos(ang)
    sin = jnp.sin(ang)
    return jnp.concatenate([cos, cos], axis=1), jnp.concatenate([-sin, sin], axis=1)


def _layer_weights(w_in, w_a2, b_a, qn_g, kn_g, w_out, w_gate, w_up, w_down):
    w_main = jnp.concatenate([w_in[:, :ALR_OFF], w_in[:, ALR_OFF + GLA_LOWRANK:]], axis=1).astype(BF16)
    w_alr = jnp.pad(w_in[:, ALR_OFF:ALR_OFF + GLA_LOWRANK], ((0, 0), (0, LANES - GLA_LOWRANK))).astype(BF16)
    w_a2p = jnp.pad(w_a2, ((0, LANES - GLA_LOWRANK), (0, 0)))
    qk_gain = jnp.concatenate([jnp.tile(qn_g, N_HEADS_DSW), jnp.tile(kn_g, N_HEADS_DSW)]).reshape(1, 2 * DSW_COLS)
    return dict(w_main=w_main, gate=(w_alr, w_a2p, b_a.reshape(1, QK_GLA_COLS)), qk_gain=qk_gain,
                w_out=w_out.astype(BF16), w_gate=w_gate.astype(BF16), w_up=w_up.astype(BF16),
                w_down=w_down.astype(BF16))


def _project_all(x2d, norm1_g, lw, pos):
    xn, log_a = _prenorm(x2d, norm1_g, gate=lw["gate"])
    cos, sin = _rope_tables(pos)
    qkv_a = _proj(xn, lw["w_main"], SEC_QKV_GLA, "plain")
    r_a = _proj(xn, lw["w_main"], SEC_R_GLA, "silu")
    qk_d = _proj(xn, lw["w_main"], SEC_QK_DSW, "rope", aux=(lw["qk_gain"], cos, sin))
    v_d = _proj(xn, lw["w_main"], SEC_V_DSW, "plain")
    return log_a, qkv_a, r_a, qk_d, v_d


def _finish(x2d, o_a, o_d, norm2_g, lw):
    h = _outproj(o_a, o_d, lw["w_out"], x2d)
    hn = _prenorm(h, norm2_g)
    u = _swiglu(hn, lw["w_gate"], lw["w_up"])
    return _down(u, lw["w_down"], h)


GLA_CHUNK_PROMPT = 128
GLA_CHUNK_SAMPLE = 16


def _prompt_layer(x2d, norm1_g, gla_norm_g, norm2_g, lw):
    t = x2d.shape[0]
    log_a, qkv_a, r_a, qk_d, v_d = _project_all(x2d, norm1_g, lw, jnp.arange(t))
    c = GLA_CHUNK_PROMPT
    nh = N_HEADS_GLA
    at = lambda blk: (lambda s, ci: (ci, blk))
    o_a, s_fin = _gla(qkv_a, qkv_a, qkv_a, log_a, r_a, gla_norm_g, c=c, ns=nh, n_groups=1, n_chunks=t // c,
                      by_cols=True, maps=(at(0), at(1), at(1), at(0), at(0), at(0)))
    o_d = _dswa_prompt(qk_d, v_d)
    y = _finish(x2d, o_a, o_d, norm2_g, lw)
    lwp = min(W_MAX, t)
    k_tail = qk_d[t - lwp:, DSW_COLS:].reshape(1, lwp, N_HEADS_DSW, HEAD_DIM)
    v_tail = v_d[t - lwp:].reshape(1, lwp, N_HEADS_DSW, HEAD_DIM)
    return y, k_tail, v_tail, s_fin[None]


def _sample_layer(x2d, nb, ls, cache_k, cache_v, state_gla, layer, norm1_g, gla_norm_g, norm2_g, lw):
    pos = PAST_LEN + jnp.tile(jnp.arange(ls), nb)
    log_a, qkv_a, r_a, qk_d, v_d = _project_all(x2d, norm1_g, lw, pos)
    c = GLA_CHUNK_SAMPLE
    nh = N_HEADS_GLA

    def seqs(a, width):
        a = a.reshape(nb, ls, nh, width).transpose(0, 2, 1, 3)
        a = jnp.pad(a, ((0, 0), (0, 0), (0, c - ls), (0, 0)))
        return a.reshape(nb * nh * c, width)

    q = seqs(qkv_a[:, :QK_GLA_COLS], DK_GLA)
    k = seqs(qkv_a[:, QK_GLA_COLS:2 * QK_GLA_COLS], DK_GLA)
    v = seqs(qkv_a[:, 2 * QK_GLA_COLS:], DV_GLA)
    g = seqs(log_a, DK_GLA)
    r = seqs(r_a, DV_GLA)
    blk = lambda s, ci: (s, 0)
    s0 = state_gla.reshape(-1, DK_GLA, DV_GLA)
    o_pad, s_fin = _gla(q, k, v, g, r, gla_norm_g, c=c, ns=nh, n_groups=nb, n_chunks=1, by_cols=False,
                        maps=(blk,) * 6, s0=s0, s0_map=lambda s, ci: (layer * nb + s, 0, 0))
    o_a = o_pad.reshape(nb, nh, c, DV_GLA)[:, :, :ls].transpose(0, 2, 1, 3).reshape(nb * ls, V_GLA_COLS)
    q_d = qk_d[:, :DSW_COLS].reshape(nb, ls, DSW_COLS)
    k_d = qk_d[:, DSW_COLS:].reshape(nb, ls, DSW_COLS)
    o_d = _dswa_sample(q_d, k_d, v_d.reshape(nb, ls, DSW_COLS), cache_k, cache_v, layer)
    o_d = o_d.reshape(nb * ls, DSW_COLS).astype(BF16)
    y = _finish(x2d, o_a, o_d, norm2_g, lw)
    k_new = k_d.reshape(nb, ls, N_HEADS_DSW, HEAD_DIM)
    v_new = v_d.reshape(nb, ls, N_HEADS_DSW, HEAD_DIM)
    return y, k_new, v_new, s_fin.reshape(nb, nh, DK_GLA, DV_GLA)


def kernel(x_prompt, x_sample, cache_k, cache_v, state_gla, norm1_g, w_in, w_a2, b_a, qn_g, kn_g,
           gla_norm_g, w_out, norm2_g, w_gate, w_up, w_down):
    bp, tp, d = x_prompt.shape
    nb, ls, _ = x_sample.shape
    assert bp == 1 and tp % DSWA_TQ == 0 and cache_k.shape[2] == W_MAX
    yp = x_prompt.reshape(tp, d)
    ys = x_sample.reshape(nb * ls, d)
    outs = [[] for _ in range(6)]
    for li in range(w_in.shape[0]):
        lw = _layer_weights(w_in[li], w_a2[li], b_a[li], qn_g[li], kn_g[li], w_out[li], w_gate[li],
                            w_up[li], w_down[li])
        yp, kp, vp, sp = _prompt_layer(yp, norm1_g[li], gla_norm_g[li], norm2_g[li], lw)
        ys, ks, vs, ss = _sample_layer(ys, nb, ls, cache_k, cache_v, state_gla, li,
                                       norm1_g[li], gla_norm_g[li], norm2_g[li], lw)
        for lst, val in zip(outs, (kp, vp, sp, ks, vs, ss)):
            lst.append(val)
    stacked = [jnp.stack(lst, axis=0) for lst in outs]
    return (yp.reshape(bp, tp, d), ys.reshape(nb, ls, d), *stacked)
```
